```python
import jax, jax.numpy as jnp
from jax import lax
import numpy as np

D_MODEL = 1024
BATCH = 16
SEQ = 4096
DEPTH = 1

MEM_LEN = 256
D_RNN = D_MODEL
LRU_HEADS = 16
LRU_HEAD_DIM = D_RNN // LRU_HEADS
LRU_CONV = 4
LRU_C = 8.0
D_CONV = D_MODEL
CONF_KERNEL = 31
N_BRANCH = 2
D_IN = 2 * D_RNN + 2 * D_CONV + N_BRANCH * D_MODEL
XA_HEADS = 4
XA_HEAD_DIM = D_MODEL // XA_HEADS
N_EXPERTS = 32
TOP_K = 4
D_EXPERT = D_MODEL
SWIGLU_ALPHA = 1.702
SWIGLU_LIMIT = 7.0
MOE_BLOCK = 256
NORM_EPS = 1e-5

kernel_name = "hybrid_rglru_conformer_xattn_moe_encoder"


def rms_norm(x, g):
    xf = x.astype(jnp.float32)
    y = xf * lax.rsqrt(jnp.mean(xf * xf, axis=-1, keepdims=True) + NORM_EPS)
    return (y * g.astype(jnp.float32)).astype(x.dtype)


def layer_norm(x, g, b):
    xf = x.astype(jnp.float32)
    mu = jnp.mean(xf, axis=-1, keepdims=True)
    xc = xf - mu
    var = jnp.mean(xc * xc, axis=-1, keepdims=True)
    y = xc * lax.rsqrt(var + NORM_EPS) * g.astype(jnp.float32) + b.astype(jnp.float32)
    return y.astype(x.dtype)


def depthwise_conv(x, w, b, pad_left, pad_right):
    C = x.shape[-1]
    y = lax.conv_general_dilated(
        x, w[:, None, :].astype(x.dtype), window_strides=(1,),
        padding=[(pad_left, pad_right)], dimension_numbers=('NWC', 'WIO', 'NWC'),
        feature_group_count=C)
    return y + b.astype(x.dtype)


def rg_lru_bidirectional(x, w_a, b_a, w_i, b_i, lam):
    B, S, _ = x.shape
    xd = jnp.stack([x, jnp.flip(x, axis=1)])
    xh = xd.reshape(2, B, S, LRU_HEADS, LRU_HEAD_DIM)
    r = jax.nn.sigmoid(jnp.einsum('zbshi,zhij->zbshj', xh, w_a).reshape(2, B, S, D_RNN)
                       + b_a[:, None, None, :])
    i = jax.nn.sigmoid(jnp.einsum('zbshi,zhij->zbshj', xh, w_i).reshape(2, B, S, D_RNN)
                       + b_i[:, None, None, :])
    log_a = (-LRU_C * jax.nn.softplus(-lam.astype(jnp.float32)))[:, None, None, :] * r.astype(jnp.float32)
    a = jnp.exp(log_a)
    u = jnp.sqrt(-jnp.expm1(2.0 * log_a)) * (i * xd).astype(jnp.float32)

    def step(h, au):
        a_t, u_t = au
        h = a_t * h + u_t
        return h, h

    _, hs = lax.scan(step, jnp.zeros((2, B, D_RNN), jnp.float32),
                     (jnp.moveaxis(a, 2, 0), jnp.moveaxis(u, 2, 0)))
    hs = jnp.moveaxis(hs, 0, 2)
    return (hs[0] + jnp.flip(hs[1], axis=1)).astype(x.dtype)


def parallel_mixer(h, w_in, b_branch_gate, lru_conv_w, lru_conv_b, lru_w_a, lru_b_a,
                   lru_w_i, lru_b_i, lru_lambda, w_rnn_out, conf_dw_w, conf_dw_b,
                   conf_ln_g, conf_ln_b, w_conf_out, w_out):
    B, S, _ = h.shape
    z = h @ w_in
    x_rnn, g_rnn, x_conf, g_br = jnp.split(
        z, [D_RNN, 2 * D_RNN, 2 * D_RNN + 2 * D_CONV], axis=-1)
    xc = depthwise_conv(x_rnn, lru_conv_w, lru_conv_b, LRU_CONV // 2, LRU_CONV - 1 - LRU_CONV // 2)
    y_a = (rg_lru_bidirectional(xc, lru_w_a, lru_b_a, lru_w_i, lru_b_i, lru_lambda)
           * jax.nn.gelu(g_rnn)) @ w_rnn_out
    c_val, c_gate = jnp.split(x_conf, 2, axis=-1)
    c = depthwise_conv(c_val * jax.nn.sigmoid(c_gate), conf_dw_w, conf_dw_b,
                       CONF_KERNEL // 2, CONF_KERNEL // 2)
    y_b = jax.nn.silu(layer_norm(c, conf_ln_g, conf_ln_b)) @ w_conf_out
    gates = jax.nn.sigmoid(g_br.reshape(B, S, N_BRANCH, D_MODEL) + b_branch_gate)
    merged = gates[:, :, 0, :] * y_a + gates[:, :, 1, :] * y_b
    return merged @ w_out


def memory_cross_attention(h, mem_n, w_q, w_kv, w_o):
    B, S, _ = h.shape
    M = mem_n.shape[1]
    q = (h @ w_q).reshape(B, S, XA_HEADS, XA_HEAD_DIM)
    k, v = jnp.split(mem_n @ w_kv, 2, axis=-1)
    k = k.reshape(B, M, XA_HEADS, XA_HEAD_DIM)
    v = v.reshape(B, M, XA_HEADS, XA_HEAD_DIM)
    s = jnp.einsum('bshd,bmhd->bhsm', q, k).astype(jnp.float32) * (XA_HEAD_DIM ** -0.5)
    p = jax.nn.softmax(s, axis=-1).astype(v.dtype)
    o = jnp.einsum('bhsm,bmhd->bshd', p, v).reshape(B, S, D_MODEL)
    return o @ w_o


def moe_ffn(h, router_w, router_b, exp_w_up, exp_b_up, exp_w_down, exp_b_down):
    B, S, D = h.shape
    N = B * S
    t = h.reshape(N, D)
    logits = (t @ router_w + router_b).astype(jnp.float32)
    top_val, top_idx = lax.top_k(logits, TOP_K)
    gate = jax.nn.softmax(top_val, axis=-1)
    A = N * TOP_K
    e_flat = top_idx.reshape(A)
    w_flat = gate.reshape(A)
    tok_flat = jnp.arange(A, dtype=jnp.int32) // TOP_K
    order = jnp.argsort(e_flat)
    e_sorted = e_flat[order]
    counts = jnp.bincount(e_flat, length=N_EXPERTS)
    start = jnp.cumsum(counts) - counts
    padded = (counts + MOE_BLOCK - 1) // MOE_BLOCK * MOE_BLOCK
    padded_end = jnp.cumsum(padded)
    padded_start = padded_end - padded
    dest = padded_start[e_sorted] + jnp.arange(A, dtype=jnp.int32) - start[e_sorted]
    n_blocks = -(-(A + N_EXPERTS * (MOE_BLOCK - 1)) // MOE_BLOCK)
    P = n_blocks * MOE_BLOCK
    row_tok = jnp.zeros((P,), jnp.int32).at[dest].set(tok_flat[order])
    row_w = jnp.zeros((P,), jnp.float32).at[dest].set(w_flat[order])
    block_start = jnp.arange(n_blocks, dtype=jnp.int32) * MOE_BLOCK
    block_exp = jnp.minimum(jnp.searchsorted(padded_end, block_start, side='right'),
                            N_EXPERTS - 1)

    def block_step(y, blk):
        tok, wt, e = blk
        gu = t[tok] @ exp_w_up[e] + exp_b_up[e]
        g, u = jnp.split(gu, 2, axis=-1)
        g = jnp.minimum(g, SWIGLU_LIMIT)
        u = jnp.clip(u, -SWIGLU_LIMIT, SWIGLU_LIMIT)
        act = (u + 1.0) * (g * jax.nn.sigmoid(SWIGLU_ALPHA * g))
        out = act @ exp_w_down[e] + exp_b_down[e]
        return y.at[tok].add(out * wt[:, None].astype(out.dtype)), None

    y, _ = lax.scan(block_step, jnp.zeros_like(t),
                    (row_tok.reshape(n_blocks, MOE_BLOCK), row_w.reshape(n_blocks, MOE_BLOCK), block_exp))
    return y.reshape(B, S, D)


def setup_inputs(seed: int = 0) -> dict:
    key = jax.random.key(seed)
    ks = jax.random.split(key, 40)
    L = DEPTH

    def nrm(k, shape, scale):
        return jax.random.normal(k, shape, jnp.float32) * scale

    a0 = jax.random.uniform(ks[30], (L, 2, D_RNN), jnp.float32, 0.9, 0.999)
    a_base = a0 ** (1.0 / LRU_C)
    lru_lambda = jnp.log(a_base) - jnp.log1p(-a_base)
    return {
        "x": nrm(ks[0], (BATCH, SEQ, D_MODEL), 1.0),
        "mem": nrm(ks[1], (BATCH, MEM_LEN, D_MODEL), 1.0),
        "norm_mix_g": 1.0 + nrm(ks[2], (L, D_MODEL), 0.02),
        "w_in": nrm(ks[3], (L, D_MODEL, D_IN), D_MODEL ** -0.5),
        "b_branch_gate": nrm(ks[4], (L, N_BRANCH, D_MODEL), 0.02),
        "lru_conv_w": nrm(ks[5], (L, LRU_CONV, D_RNN), LRU_CONV ** -0.5),
        "lru_conv_b": nrm(ks[6], (L, D_RNN), 0.02),
        "lru_w_a": nrm(ks[7], (L, 2, LRU_HEADS, LRU_HEAD_DIM, LRU_HEAD_DIM), LRU_HEAD_DIM ** -0.5),
        "lru_b_a": nrm(ks[8], (L, 2, D_RNN), 0.02),
        "lru_w_i": nrm(ks[9], (L, 2, LRU_HEADS, LRU_HEAD_DIM, LRU_HEAD_DIM), LRU_HEAD_DIM ** -0.5),
        "lru_b_i": nrm(ks[10], (L, 2, D_RNN), 0.02),
        "lru_lambda": lru_lambda,
        "w_rnn_out": nrm(ks[11], (L, D_RNN, D_MODEL), D_RNN ** -0.5),
        "conf_dw_w": nrm(ks[12], (L, CONF_KERNEL, D_CONV), CONF_KERNEL ** -0.5),
        "conf_dw_b": nrm(ks[13], (L, D_CONV), 0.02),
        "conf_ln_g": 1.0 + nrm(ks[14], (L, D_CONV), 0.02),
        "conf_ln_b": nrm(ks[15], (L, D_CONV), 0.02),
        "w_conf_out": nrm(ks[16], (L, D_CONV, D_MODEL), D_CONV ** -0.5),
        "w_out": nrm(ks[17], (L, D_MODEL, D_MODEL), D_MODEL ** -0.5),
        "norm_xa_g": 1.0 + nrm(ks[18], (L, D_MODEL), 0.02),
        "norm_mem_g": 1.0 + nrm(ks[19], (L, D_MODEL), 0.02),
        "xa_w_q": nrm(ks[20], (L, D_MODEL, D_MODEL), D_MODEL ** -0.5),
        "xa_w_kv": nrm(ks[21], (L, D_MODEL, 2 * D_MODEL), D_MODEL ** -0.5),
        "xa_w_o": nrm(ks[22], (L, D_MODEL, D_MODEL), D_MODEL ** -0.5),
        "norm_ffn_g": 1.0 + nrm(ks[23], (L, D_MODEL), 0.02),
        "router_w": nrm(ks[24], (L, D_MODEL, N_EXPERTS), D_MODEL ** -0.5),
        "router_b": nrm(ks[25], (L, N_EXPERTS), 0.01),
        "exp_w_up": nrm(ks[26], (L, N_EXPERTS, D_MODEL, 2 * D_EXPERT), D_MODEL ** -0.5),
        "exp_b_up": nrm(ks[27], (L, N_EXPERTS, 2 * D_EXPERT), 0.01),
        "exp_w_down": nrm(ks[28], (L, N_EXPERTS, D_EXPERT, D_MODEL), D_EXPERT ** -0.5),
        "exp_b_down": nrm(ks[29], (L, N_EXPERTS, D_MODEL), 0.01),
        "final_norm_g": 1.0 + nrm(ks[31], (D_MODEL,), 0.02),
    }


def reference(x, mem, norm_mix_g, w_in, b_branch_gate, lru_conv_w, lru_conv_b, lru_w_a,
              lru_b_a, lru_w_i, lru_b_i, lru_lambda, w_rnn_out, conf_dw_w, conf_dw_b,
              conf_ln_g, conf_ln_b, w_conf_out, w_out, norm_xa_g, norm_mem_g, xa_w_q,
              xa_w_kv, xa_w_o, norm_ffn_g, router_w, router_b, exp_w_up, exp_b_up,
              exp_w_down, exp_b_down, final_norm_g):
    for l in range(DEPTH):
        h = rms_norm(x, norm_mix_g[l])
        x = x + parallel_mixer(h, w_in[l], b_branch_gate[l], lru_conv_w[l], lru_conv_b[l],
                               lru_w_a[l], lru_b_a[l], lru_w_i[l], lru_b_i[l], lru_lambda[l],
                               w_rnn_out[l], conf_dw_w[l], conf_dw_b[l], conf_ln_g[l],
                               conf_ln_b[l], w_conf_out[l], w_out[l])
        h = rms_norm(x, norm_xa_g[l])
        x = x + memory_cross_attention(h, rms_norm(mem, norm_mem_g[l]),
                                       xa_w_q[l], xa_w_kv[l], xa_w_o[l])
        h = rms_norm(x, norm_ffn_g[l])
        x = x + moe_ffn(h, router_w[l], router_b[l], exp_w_up[l], exp_b_up[l],
                        exp_w_down[l], exp_b_down[l])
    return rms_norm(x, final_norm_g)
```

```python
import functools

import jax
import jax.numpy as jnp
from jax import lax
from jax.experimental import pallas as pl
from jax.experimental.pallas import tpu as pltpu

NORM_EPS = 1e-5
LRU_HEADS = 16
LRU_C = 8.0
XA_HEADS = 4
TOP_K = 4
SWIGLU_ALPHA = 1.702
SWIGLU_LIMIT = 7.0

LANES = 128
VMEM_LIMIT_BYTES = 56 * 1024 * 1024

F32 = jnp.float32
BF16 = jnp.bfloat16


def _params(*sem):
    return pltpu.CompilerParams(dimension_semantics=sem, vmem_limit_bytes=VMEM_LIMIT_BYTES)


def _blk(n, pref):
    b = min(n, pref)
    while n % b:
        b -= 1
    return b


def _rms(x, g):
    return x * lax.rsqrt(jnp.mean(x * x, axis=-1, keepdims=True) + NORM_EPS) * g


def _pack_bf16_pairs(x):
    w = x.shape[-1] // 2
    bits = lax.bitcast_convert_type(x.astype(BF16).astype(F32), jnp.uint32)
    return (bits[:, w:] & jnp.uint32(0xFFFF0000)) | (bits[:, :w] >> 16)


def _unpack_bf16_pairs(p):
    lo = lax.bitcast_convert_type(p << 16, F32)
    hi = lax.bitcast_convert_type(p & jnp.uint32(0xFFFF0000), F32)
    return lo, hi


def _kv_body(m_ref, g_ref, w_ref, o_ref):
    h = _rms(m_ref[...], g_ref[...])
    o_ref[...] = jnp.dot(h.astype(BF16), w_ref[...], preferred_element_type=F32).astype(BF16)


def _kv_proj(mem2d, g, w_kv):
    rows, d = mem2d.shape
    tr = _blk(rows, 512)
    return pl.pallas_call(
        _kv_body,
        grid=(rows // tr,),
        in_specs=[pl.BlockSpec((tr, d), lambda i: (i, 0)),
                  pl.BlockSpec((1, d), lambda i: (0, 0)),
                  pl.BlockSpec(w_kv.shape, lambda i: (0, 0))],
        out_specs=pl.BlockSpec((tr, w_kv.shape[1]), lambda i: (i, 0)),
        out_shape=jax.ShapeDtypeStruct((rows, w_kv.shape[1]), BF16),
        compiler_params=_params("parallel"),
        name="kv_proj",
    )(mem2d, g, w_kv)


def _inproj_body(x_ref, g_ref, w_ref, o_ref, h_ref):
    @pl.when(pl.program_id(2) == 0)
    def _():
        h_ref[...] = _rms(x_ref[...], g_ref[...]).astype(BF16)

    o_ref[...] = jnp.dot(h_ref[...], w_ref[...], preferred_element_type=F32).astype(BF16)


def _in_proj(x, g, w_in):
    b, s, d = x.shape
    din = w_in.shape[1]
    t = _blk(s, 512)
    tn = _blk(din, 2048)
    nj = din // tn
    return pl.pallas_call(
        _inproj_body,
        grid=(b, s // t, nj),
        in_specs=[pl.BlockSpec((None, t, d), lambda bi, i, j: (bi, i, 0)),
                  pl.BlockSpec((1, d), lambda bi, i, j: (0, 0)),
                  pl.BlockSpec((d, tn), lambda bi, i, j: (0, j))],
        out_specs=pl.BlockSpec((t, tn), lambda bi, i, j: (i, bi * nj + j)),
        out_shape=jax.ShapeDtypeStruct((s, b * din), BF16),
        scratch_shapes=[pltpu.VMEM((t, d), BF16)],
        compiler_params=_params("parallel", "parallel", "arbitrary"),
        name="in_proj",
    )(x, g, w_in)


def _conf_body(cv_ref, cg_ref, cvp_ref, cgp_ref, cvn_ref, cgn_ref, w_ref, b_ref, lg_ref, lb_ref,
               o_ref, glu_ref, *, t_blk, halo, width):
    i = pl.program_id(0)
    n = pl.num_programs(0)

    def glu(v_ref, g_ref):
        return v_ref[...].astype(F32) * jax.nn.sigmoid(g_ref[...].astype(F32))

    glu_ref[0:halo] = jnp.where(i > 0, glu(cvp_ref, cgp_ref), 0.0)
    glu_ref[halo:halo + t_blk] = glu(cv_ref, cg_ref)
    glu_ref[halo + t_blk:halo + t_blk + halo] = jnp.where(i < n - 1, glu(cvn_ref, cgn_ref), 0.0)

    off = halo - width // 2
    bias = b_ref[...]
    ln_g = lg_ref[...]
    ln_b = lb_ref[...]

    def step(t, carry):
        acc = w_ref[0:1, :] * glu_ref[t + off]
        for k in range(1, width):
            acc = acc + w_ref[k:k + 1, :] * glu_ref[t + off + k]
        acc = acc + bias
        mu = jnp.mean(acc, axis=-1, keepdims=True)
        xc = acc - mu
        var = jnp.mean(xc * xc, axis=-1, keepdims=True)
        y = xc * lax.rsqrt(var + NORM_EPS) * ln_g + ln_b
        o_ref[t] = (y * jax.nn.sigmoid(y)).astype(o_ref.dtype)
        return carry

    lax.fori_loop(0, t_blk, step, 0)


def _conformer_branch(z3, dw_w, dw_b, ln_g, ln_b, *, col_val, col_gate):
    s, b, _ = z3.shape
    width, c = dw_w.shape
    halo = 16
    assert width // 2 <= halo
    t = _blk(s, 64)
    assert t % halo == 0
    r = t // halo
    nh = s // halo
    main = lambda col: pl.BlockSpec((t, b, c), lambda i: (i, 0, col))
    prev = lambda col: pl.BlockSpec((halo, b, c), lambda i: (jnp.maximum(i * r - 1, 0), 0, col))
    nxt = lambda col: pl.BlockSpec((halo, b, c), lambda i: (jnp.minimum((i + 1) * r, nh - 1), 0, col))
    vec = lambda rows: pl.BlockSpec((rows, c), lambda i: (0, 0))
    return pl.pallas_call(
        functools.partial(_conf_body, t_blk=t, halo=halo, width=width),
        grid=(s // t,),
        in_specs=[main(col_val), main(col_gate), prev(col_val), prev(col_gate), nxt(col_val), nxt(col_gate),
                  vec(width), vec(1), vec(1), vec(1)],
        out_specs=pl.BlockSpec((t, b, c), lambda i: (i, 0, 0)),
        out_shape=jax.ShapeDtypeStruct((s, b, c), BF16),
        scratch_shapes=[pltpu.VMEM((t + 2 * halo, b, c), F32)],
        compiler_params=_params("parallel"),
        name="conformer_branch",
    )(z3, z3, z3, z3, z3, z3, dw_w, dw_b, ln_g, ln_b)


def _lru_body(x_ref, xp_ref, xn_ref, cw_ref, cb_ref, wg_ref, ba_ref, bi_ref, lam_ref, *rest,
              t_blk, rev, pad_left):
    if rev:
        hf_ref, g_ref, o_ref, xw_ref, a_ref, u_ref, h_ref = rest
    else:
        o_ref, xw_ref, a_ref, u_ref, h_ref = rest
    i = pl.program_id(1)
    n = pl.num_programs(1)
    ti = (n - 1 - i) if rev else i
    nb, cb = h_ref.shape
    width = cw_ref.shape[0]

    @pl.when(i == 0)
    def _():
        h_ref[...] = jnp.zeros_like(h_ref)

    xw_ref[0:pad_left] = jnp.where(ti > 0, xp_ref[...].astype(F32), 0.0)
    xw_ref[pad_left:pad_left + t_blk] = x_ref[...].astype(F32)
    xw_ref[pad_left + t_blk:] = jnp.where(ti < n - 1, xn_ref[...].astype(F32), 0.0)

    xc = cb_ref[...] + cw_ref[0:1, :] * xw_ref[0:t_blk]
    for k in range(1, width):
        xc = xc + cw_ref[k:k + 1, :] * xw_ref[k:k + t_blk]
    x2 = xc.reshape(t_blk * nb, cb)
    pre = jnp.dot(x2.astype(BF16), wg_ref[...], preferred_element_type=F32)
    r = jax.nn.sigmoid(pre[:, :cb] + ba_ref[...])
    ig = jax.nn.sigmoid(pre[:, cb:] + bi_ref[...])
    nl = -lam_ref[...]
    softplus = jnp.maximum(nl, 0.0) + jnp.log1p(jnp.exp(-jnp.abs(nl)))
    log_a = (-LRU_C * softplus) * r
    a = jnp.exp(log_a)
    u = jnp.sqrt(-jnp.tanh(log_a) * (a * a + 1.0)) * (ig * x2)
    a_ref[...] = a.reshape(t_blk, nb, cb)
    u_ref[...] = u.reshape(t_blk, nb, cb)

    def step(sidx, h):
        t = (t_blk - 1 - sidx) if rev else sidx
        h = a_ref[t] * h + u_ref[t]
        u_ref[t] = h
        return h

    h_ref[...] = lax.fori_loop(0, t_blk, step, h_ref[...], unroll=8)

    if rev:
        hs = hf_ref[...].astype(F32) + u_ref[...]
        o_ref[...] = (hs * jax.nn.gelu(g_ref[...].astype(F32))).astype(o_ref.dtype)
    else:
        o_ref[...] = u_ref[...].astype(o_ref.dtype)


def _lru_pass(z3, conv_w, conv_b, wg, b_a, b_i, lam, hs_fwd, *, rev, col_x, col_g, c):
    s, b, _ = z3.shape
    ncb, cb, _ = wg.shape
    width = conv_w.shape[0]
    pad_left = width // 2
    pad_right = width - 1 - pad_left
    t = _blk(s, 64)
    assert t % pad_left == 0 and pad_right == 1
    nt = s // t
    tmap = (lambda i: nt - 1 - i) if rev else (lambda i: i)
    cbx = col_x * (c // cb)
    cbg = col_g * (c // cb)
    in_specs = [
        pl.BlockSpec((t, b, cb), lambda j, i: (tmap(i), 0, cbx + j)),
        pl.BlockSpec((pad_left, b, cb),
                     lambda j, i: (jnp.maximum(tmap(i) * (t // pad_left) - 1, 0), 0, cbx + j)),
        pl.BlockSpec((pad_right, b, cb), lambda j, i: (jnp.minimum((tmap(i) + 1) * t, s - 1), 0, cbx + j)),
        pl.BlockSpec((width, cb), lambda j, i: (0, j)),
        pl.BlockSpec((1, cb), lambda j, i: (0, j)),
        pl.BlockSpec((None, cb, 2 * cb), lambda j, i: (j, 0, 0)),
        pl.BlockSpec((1, cb), lambda j, i: (0, j)),
        pl.BlockSpec((1, cb), lambda j, i: (0, j)),
        pl.BlockSpec((1, cb), lambda j, i: (0, j)),
    ]
    args = [z3, z3, z3, conv_w, conv_b, wg, b_a, b_i, lam]
    if rev:
        in_specs += [pl.BlockSpec((t, b, cb), lambda j, i: (tmap(i), 0, j)),
                     pl.BlockSpec((t, b, cb), lambda j, i: (tmap(i), 0, cbg + j))]
        args += [hs_fwd, z3]
    return pl.pallas_call(
        functools.partial(_lru_body, t_blk=t, rev=rev, pad_left=pad_left),
        grid=(ncb, nt),
        in_specs=in_specs,
        out_specs=pl.BlockSpec((t, b, cb), lambda j, i: (tmap(i), 0, j)),
        out_shape=jax.ShapeDtypeStruct((s, b, c), BF16),
        scratch_shapes=[pltpu.VMEM((t + width - 1, b, cb), F32),
                        pltpu.VMEM((t, b, cb), F32),
                        pltpu.VMEM((t, b, cb), F32),
                        pltpu.VMEM((b, cb), F32)],
        compiler_params=_params("parallel", "arbitrary"),
        name="rg_lru_rev" if rev else "rg_lru_fwd",
    )(*args)


def _post_body(ya_ref, yb_ref, ga_ref, gb_ref, x_ref, kt_ref, v_ref, wr_ref, wc_ref, wo_ref, wq_ref, wxo_ref,
               bg_ref, gxa_ref, gffn_ref, rw_ref, rb_ref,
               x2_ref, h3_ref, idx_ref, gate_ref, *, heads):
    d = x_ref.shape[-1]
    hd = d // heads
    mm = lambda a, w: jnp.dot(a.astype(BF16), w, preferred_element_type=F32)

    y_a = mm(ya_ref[...], wr_ref[...])
    y_b = mm(yb_ref[...], wc_ref[...])
    gate_a = jax.nn.sigmoid(ga_ref[...].astype(F32) + bg_ref[0:1, :])
    gate_b = jax.nn.sigmoid(gb_ref[...].astype(F32) + bg_ref[1:2, :])
    x1 = x_ref[...] + mm(gate_a * y_a + gate_b * y_b, wo_ref[...])

    q = mm(_rms(x1, gxa_ref[...]), wq_ref[...]) * (hd ** -0.5)
    outs = []
    for h in range(heads):
        sl = slice(h * hd, (h + 1) * hd)
        sc = mm(q[:, sl], kt_ref[sl, :])
        p = jnp.exp(sc - jnp.max(sc, axis=-1, keepdims=True))
        p = p / jnp.sum(p, axis=-1, keepdims=True)
        outs.append(mm(p, v_ref[:, sl]))
    x2 = x1 + mm(jnp.concatenate(outs, axis=-1), wxo_ref[...])
    x2_ref[...] = x2

    h3 = _rms(x2, gffn_ref[...])
    h3_ref[...] = _pack_bf16_pairs(h3)

    n_exp = rw_ref.shape[1]
    logits = jnp.dot(h3, rw_ref[...], preferred_element_type=F32,
                     precision=lax.Precision.HIGHEST) + rb_ref[...]
    rows = logits.shape[0]
    e_iota = lax.broadcasted_iota(jnp.int32, (rows, n_exp), 1).astype(F32)
    lane = lax.broadcasted_iota(jnp.int32, (rows, LANES), 1)
    vals = logits
    top_v, top_i = [], []
    for _ in range(TOP_K):
        m = jnp.max(vals, axis=-1, keepdims=True)
        sel = jnp.min(jnp.where(vals == m, e_iota, float(n_exp)), axis=-1, keepdims=True)
        top_v.append(m)
        top_i.append(sel.astype(jnp.int32))
        vals = jnp.where(e_iota == sel, -jnp.inf, vals)
    exps = [jnp.exp(v - top_v[0]) for v in top_v]
    den = exps[0]
    for e in exps[1:]:
        den = den + e
    idx_out = jnp.zeros((rows, LANES), jnp.int32)
    gate_out = jnp.zeros((rows, LANES), F32)
    for k in range(TOP_K):
        idx_out = jnp.where(lane == k, top_i[k], idx_out)
        gate_out = jnp.where(lane == k, exps[k] / den, gate_out)
    idx_ref[...] = idx_out
    gate_ref[...] = gate_out


def _post_mixer(ya2, yb2, z2, x, kt, v, w_rnn_out, w_conf_out, w_out, w_q, w_xo, b_gate, g_xa, g_ffn,
                router_w, router_b, *, col_ga, col_gb):
    b, s, d = x.shape
    m = v.shape[1]
    n_exp = router_w.shape[1]
    t = _blk(s, 256)
    ncol = z2.shape[1] // (b * d)
    const = lambda shape: pl.BlockSpec(shape, lambda bi, i: tuple(0 for _ in shape))
    row_out = lambda w: pl.BlockSpec((None, t, w), lambda bi, i: (bi, i, 0))
    return pl.pallas_call(
        functools.partial(_post_body, heads=XA_HEADS),
        grid=(b, s // t),
        in_specs=[pl.BlockSpec((t, d), lambda bi, i: (i, bi)),
                  pl.BlockSpec((t, d), lambda bi, i: (i, bi)),
                  pl.BlockSpec((t, d), lambda bi, i: (i, bi * ncol + col_ga)),
                  pl.BlockSpec((t, d), lambda bi, i: (i, bi * ncol + col_gb)),
                  pl.BlockSpec((None, t, d), lambda bi, i: (bi, i, 0)),
                  pl.BlockSpec((None, d, m), lambda bi, i: (bi, 0, 0)),
                  pl.BlockSpec((None, m, d), lambda bi, i: (bi, 0, 0)),
                  const((d, d)), const((d, d)), const((d, d)), const((d, d)), const((d, d)),
                  const((2, d)), const((1, d)), const((1, d)), const((d, n_exp)), const((1, n_exp))],
        out_specs=[row_out(d), row_out(d // 2), row_out(LANES), row_out(LANES)],
        out_shape=[jax.ShapeDtypeStruct((b, s, d), F32),
                   jax.ShapeDtypeStruct((b, s, d // 2), jnp.uint32),
                   jax.ShapeDtypeStruct((b, s, LANES), jnp.int32),
                   jax.ShapeDtypeStruct((b, s, LANES), F32)],
        compiler_params=_params("parallel", "parallel"),
        name="post_mixer",
    )(ya2, yb2, z2, z2, x, kt, v, w_rnn_out, w_conf_out, w_out, w_q, w_xo, b_gate, g_xa, g_ffn,
      router_w, router_b)


def _moe_body(texp_ref, nused_ref, nvalid_ref, tokc_ref, tokn_ref, slot_ref, h3_hbm, wup_ref, bup_ref, wdn_ref,
              bdn_ref, y_hbm, xbuf, obuf, gsem, ssem, *, tm):
    del texp_ref
    i = pl.program_id(0)
    n_used = nused_ref[0]
    cur = i % 2
    half = wup_ref.shape[0] // 2
    f = wdn_ref.shape[0]

    def gather_copy(tok, j, slot):
        return pltpu.make_async_copy(h3_hbm.at[pl.ds(tok, 1)], xbuf.at[slot, pl.ds(j, 1)], gsem.at[slot])

    def scatter_copy(row, j, slot):
        return pltpu.make_async_copy(obuf.at[slot, pl.ds(j, 1)], y_hbm.at[pl.ds(row, 1)], ssem.at[slot])

    def start_gather(tok_ref, slot):
        def body(j, c):
            gather_copy(tok_ref[0, 0, j], j, slot).start()
            return c
        lax.fori_loop(0, tm, body, 0, unroll=8)

    def wait_gather(slot):
        pltpu.make_async_copy(h3_hbm.at[pl.ds(0, tm)], xbuf.at[slot], gsem.at[slot]).wait()

    def start_scatter(slot, count):
        def body(j, c):
            scatter_copy(slot_ref[0, 0, j], j, slot).start()
            return c

        @pl.when(count == tm)
        def _():
            lax.fori_loop(0, tm, body, 0, unroll=8)

        @pl.when(count < tm)
        def _():
            lax.fori_loop(0, count, body, 0)

    def wait_scatter(slot, count):
        @pl.when(count == tm)
        def _():
            pltpu.make_async_copy(obuf.at[slot], y_hbm.at[pl.ds(0, tm)], ssem.at[slot]).wait()

        @pl.when(count < tm)
        def _():
            def body(j, c):
                scatter_copy(0, 0, slot).wait()
                return c
            lax.fori_loop(0, count, body, 0)

    @pl.when(i == 0)
    def _():
        start_gather(tokc_ref, 0)

    @pl.when(i + 1 < n_used)
    def _():
        start_gather(tokn_ref, 1 - cur)

    @pl.when(i < n_used)
    def _():
        wait_gather(cur)
        lo, hi = _unpack_bf16_pairs(xbuf[cur])
        gu = (jnp.dot(lo.astype(BF16), wup_ref[:half, :], preferred_element_type=F32)
              + jnp.dot(hi.astype(BF16), wup_ref[half:, :], preferred_element_type=F32)
              + bup_ref[...])
        g = jnp.minimum(gu[:, :f], SWIGLU_LIMIT)
        u = jnp.clip(gu[:, f:], -SWIGLU_LIMIT, SWIGLU_LIMIT)
        act = (u + 1.0) * (g * jax.nn.sigmoid(SWIGLU_ALPHA * g))
        out = jnp.dot(act.astype(BF16), wdn_ref[...], preferred_element_type=F32) + bdn_ref[...]
        obuf[cur] = _pack_bf16_pairs(out)
        start_scatter(cur, nvalid_ref[i])

        @pl.when(i >= 1)
        def _():
            wait_scatter(1 - cur, nvalid_ref[jnp.maximum(i - 1, 0)])

        @pl.when(i == n_used - 1)
        def _():
            wait_scatter(cur, nvalid_ref[i])


def _moe_experts(h3p, row_tok, row_slot, tile_exp, n_used, tile_valid, w_up, b_up, w_dn, b_dn, *, tm, n_slots):
    n_tiles = row_tok.shape[0]
    e, d, f2 = w_up.shape
    f = f2 // 2
    dh = h3p.shape[1]
    smem = lambda imap: pl.BlockSpec((1, 1, tm), imap, memory_space=pltpu.SMEM)
    grid_spec = pltpu.PrefetchScalarGridSpec(
        num_scalar_prefetch=3,
        grid=(n_tiles,),
        in_specs=[smem(lambda i, te, nu, nv: (i, 0, 0)),
                  smem(lambda i, te, nu, nv: (jnp.minimum(i + 1, n_tiles - 1), 0, 0)),
                  smem(lambda i, te, nu, nv: (i, 0, 0)),
                  pl.BlockSpec(memory_space=pl.ANY),
                  pl.BlockSpec((None, d, f2), lambda i, te, nu, nv: (te[i], 0, 0)),
                  pl.BlockSpec((None, 1, f2), lambda i, te, nu, nv: (te[i], 0, 0)),
                  pl.BlockSpec((None, f, d), lambda i, te, nu, nv: (te[i], 0, 0)),
                  pl.BlockSpec((None, 1, d), lambda i, te, nu, nv: (te[i], 0, 0))],
        out_specs=pl.BlockSpec(memory_space=pl.ANY),
        scratch_shapes=[pltpu.VMEM((2, tm, dh), jnp.uint32),
                        pltpu.VMEM((2, tm, dh), jnp.uint32),
                        pltpu.SemaphoreType.DMA((2,)),
                        pltpu.SemaphoreType.DMA((2,))])
    return pl.pallas_call(
        functools.partial(_moe_body, tm=tm),
        grid_spec=grid_spec,
        out_shape=jax.ShapeDtypeStruct((n_slots, dh), jnp.uint32),
        compiler_params=_params("arbitrary"),
        name="moe_experts",
    )(tile_exp, n_used, tile_valid, row_tok, row_tok, row_slot, h3p, w_up, b_up, w_dn, b_dn)


def _combine_body(*refs):
    y_refs = refs[:TOP_K]
    gate_ref, x2_ref, g_ref, o_ref = refs[TOP_K:]
    w = x2_ref.shape[-1] // 2
    gate = gate_ref[...]
    lo, hi = x2_ref[:, :w], x2_ref[:, w:]
    for k in range(TOP_K):
        ylo, yhi = _unpack_bf16_pairs(y_refs[k][...])
        lo = lo + gate[:, k:k + 1] * ylo
        hi = hi + gate[:, k:k + 1] * yhi
    ms = (jnp.sum(lo * lo, axis=-1, keepdims=True) + jnp.sum(hi * hi, axis=-1, keepdims=True)) / (2 * w)
    scale = lax.rsqrt(ms + NORM_EPS)
    o_ref[:, :w] = lo * scale * g_ref[:, :w]
    o_ref[:, w:] = hi * scale * g_ref[:, w:]


def _combine(ybuf, gate, x2, g_final):
    n, d = x2.shape
    tb = _blk(n, 512)
    nb = n // tb
    y_spec = lambda k: pl.BlockSpec((tb, d // 2), lambda i: (k * nb + i, 0))
    return pl.pallas_call(
        _combine_body,
        grid=(nb,),
        in_specs=[y_spec(k) for k in range(TOP_K)] + [
            pl.BlockSpec((tb, LANES), lambda i: (i, 0)),
            pl.BlockSpec((tb, d), lambda i: (i, 0)),
            pl.BlockSpec((1, d), lambda i: (0, 0))],
        out_specs=pl.BlockSpec((tb, d), lambda i: (i, 0)),
        out_shape=jax.ShapeDtypeStruct((n, d), F32),
        compiler_params=_params("parallel"),
        name="moe_combine",
    )(*([ybuf] * TOP_K), gate, x2, g_final)


def _gate_weights(w_a, w_i, cb):
    h, dh, _ = w_a.shape
    per = cb // dh

    def blockdiag(w):
        w = w.reshape(h // per, per, dh, dh)
        eye = jnp.eye(per, dtype=w.dtype)
        return jnp.einsum('gpij,pq->gpiqj', w, eye).reshape(h // per, cb, cb)

    return jnp.concatenate([blockdiag(w_a), blockdiag(w_i)], axis=-1).astype(BF16)


def _moe_tile_rows(n_assign, n_exp):
    return 256 if n_assign // n_exp >= 1024 else 64


def _layer(x, mem, norm_mix_g, w_in, b_branch_gate, lru_conv_w, lru_conv_b, lru_w_a, lru_b_a, lru_w_i,
           lru_b_i, lru_lambda, w_rnn_out, conf_dw_w, conf_dw_b, conf_ln_g, conf_ln_b, w_conf_out, w_out,
           norm_xa_g, norm_mem_g, xa_w_q, xa_w_kv, xa_w_o, norm_ffn_g, router_w, router_b, exp_w_up,
           exp_b_up, exp_w_down, exp_b_down, final_g):
    b, s, d = x.shape
    m = mem.shape[1]
    n = b * s
    c = w_rnn_out.shape[0]
    n_exp = router_w.shape[1]
    row = lambda v: v.reshape(1, -1)

    kv = _kv_proj(mem.reshape(b * m, d), row(norm_mem_g), xa_w_kv.astype(BF16)).reshape(b, m, 2 * d)
    kt = jnp.swapaxes(kv[:, :, :d], 1, 2)
    v = kv[:, :, d:]

    z2 = _in_proj(x, row(norm_mix_g), w_in.astype(BF16))
    z3 = z2.reshape(s, b, -1)
    yb = _conformer_branch(z3, conf_dw_w, row(conf_dw_b), row(conf_ln_g), row(conf_ln_b), col_val=2, col_gate=3)
    cb = min(c, 256)
    hs = None
    for direction in (0, 1):
        hs = _lru_pass(z3, lru_conv_w, row(lru_conv_b),
                       _gate_weights(lru_w_a[direction], lru_w_i[direction], cb),
                       row(lru_b_a[direction]), row(lru_b_i[direction]), row(lru_lambda[direction]),
                       hs, rev=bool(direction), col_x=0, col_g=1, c=c)
    x2, h3p, top_idx, top_gate = _post_mixer(
        hs.reshape(s, b * c), yb.reshape(s, b * c), z2, x, kt, v,
        w_rnn_out.astype(BF16), w_conf_out.astype(BF16), w_out.astype(BF16), xa_w_q.astype(BF16),
        xa_w_o.astype(BF16), b_branch_gate, row(norm_xa_g), row(norm_ffn_g), router_w, row(router_b),
        col_ga=4, col_gb=5)

    n_assign = n * TOP_K
    tm = _moe_tile_rows(n_assign, n_exp)
    n_tiles = -(-(n_assign + n_exp * (tm - 1)) // tm)
    e_flat = top_idx.reshape(n, LANES)[:, :TOP_K].T.reshape(-1)
    order = jnp.argsort(e_flat).astype(jnp.int32)
    counts = jnp.sum(e_flat[:, None] == jnp.arange(n_exp, dtype=jnp.int32)[None, :], axis=0, dtype=jnp.int32)
    start = jnp.cumsum(counts) - counts
    tiles_per = (counts + tm - 1) // tm
    tile_end = jnp.cumsum(tiles_per)
    tile_start = tile_end - tiles_per
    tile_exp = jnp.minimum(jnp.searchsorted(tile_end, jnp.arange(n_tiles, dtype=jnp.int32), side='right'),
                           n_exp - 1).astype(jnp.int32)
    within = (jnp.arange(n_tiles, dtype=jnp.int32) - tile_start[tile_exp])[:, None] * tm \
        + jnp.arange(tm, dtype=jnp.int32)[None, :]
    valid = within < counts[tile_exp][:, None]
    rank = jnp.minimum(start[tile_exp][:, None] + within, n_assign - 1)
    assign = order[rank]
    row_slot = jnp.where(valid, assign, 0).reshape(n_tiles, 1, tm)
    row_tok = jnp.where(valid, assign % n, 0).reshape(n_tiles, 1, tm)
    tile_valid = jnp.sum(valid, axis=1, dtype=jnp.int32)
    n_used = tile_end[-1:].astype(jnp.int32)

    ybuf = _moe_experts(h3p.reshape(n, d // 2), row_tok, row_slot, tile_exp, n_used, tile_valid,
                        exp_w_up.astype(BF16), exp_b_up[:, None, :], exp_w_down.astype(BF16),
                        exp_b_down[:, None, :], tm=tm, n_slots=n_assign)
    out = _combine(ybuf, top_gate.reshape(n, LANES), x2.reshape(n, d), row(final_g))
    return out.reshape(b, s, d)


def kernel(x, mem, norm_mix_g, w_in, b_branch_gate, lru_conv_w, lru_conv_b, lru_w_a, lru_b_a, lru_w_i, lru_b_i, lru_lambda, w_rnn_out, conf_dw_w, conf_dw_b, conf_ln_g, conf_ln_b, w_conf_out, w_out, norm_xa_g, norm_mem_g, xa_w_q, xa_w_kv, xa_w_o, norm_ffn_g, router_w, router_b, exp_w_up, exp_b_up, exp_w_down, exp_b_down, final_norm_g):
    depth = w_in.shape[0]
    assert depth == 1, "the final norm is fused into the last layer's expert combine"
    l = 0
    return _layer(x, mem, norm_mix_g[l], w_in[l], b_branch_gate[l], lru_conv_w[l], lru_conv_b[l],
                  lru_w_a[l], lru_b_a[l], lru_w_i[l], lru_b_i[l], lru_lambda[l], w_rnn_out[l],
                  conf_dw_w[l], conf_dw_b[l], conf_ln_g[l], conf_ln_b[l], w_conf_out[l], w_out[l],
                  norm_xa_g[l], norm_mem_g[l], xa_w_q[l], xa_w_kv[l], xa_w_o[l], norm_ffn_g[l],
                  router_w[l], router_b[l], exp_w_up[l], exp_b_up[l], exp_w_down[l], exp_b_down[l],
                  final_norm_g)
```
